```python
import jax, jax.numpy as jnp
from jax import lax
import numpy as np

D_MODEL = 1024
BATCH = 8
SEQ = 2048
DEPTH = 4
DEC_BATCH = 128
DEC_SEQ = 1
PAST_LEN = 16384
PAGE_SIZE = 128

N_META = 16
D_A = D_MODEL
D_B = D_MODEL
CONV_A = 31
CONV_B = 3
BUF_A = min(CONV_A - 1, PAST_LEN)
BUF_B = min(CONV_B - 1, PAST_LEN)
RMS_EPS = 1e-6
LN_EPS = 1e-5

D_IN = 3 * D_A + 4 * D_B + 2 * D_MODEL
SPLIT_IDX = (
    D_A,
    2 * D_A,
    3 * D_A,
    3 * D_A + D_B,
    3 * D_A + 2 * D_B,
    3 * D_A + 3 * D_B,
    3 * D_A + 4 * D_B,
    3 * D_A + 4 * D_B + D_MODEL,
)

kernel_name = "hybrid_conformer_shortconv_gated_decoder_step"


def rmsnorm(x, g):
    xf = x.astype(jnp.float32)
    y = xf * lax.rsqrt(jnp.mean(xf * xf, axis=-1, keepdims=True) + RMS_EPS)
    return (y * g.astype(jnp.float32)).astype(x.dtype)


def layernorm(x, g, b):
    xf = x.astype(jnp.float32)
    mu = jnp.mean(xf, axis=-1, keepdims=True)
    xc = xf - mu
    var = jnp.mean(xc * xc, axis=-1, keepdims=True)
    y = xc * lax.rsqrt(var + LN_EPS)
    return (y * g.astype(jnp.float32) + b.astype(jnp.float32)).astype(x.dtype)


def causal_dwconv(buf, u, w):
    pad = jnp.concatenate([buf.astype(u.dtype), u], axis=1)
    c = u.shape[-1]
    out = lax.conv_general_dilated(
        pad, w.astype(u.dtype)[:, None, :],
        window_strides=(1,), padding='VALID',
        dimension_numbers=('NWC', 'WIO', 'NWC'),
        feature_group_count=c)
    new_buf = pad[:, pad.shape[1] - buf.shape[1]:, :]
    return out, new_buf


def mixer_layer(h, buf_a, buf_b, norm_g, w_in, conv_a_w, conv_a_b, ln_a_g, ln_a_b,
                conv_b_w, w_pa, w_pb, w_out):
    z = rmsnorm(h, norm_g)
    proj = jnp.einsum('btd,de->bte', z, w_in)
    a_val, a_glu, a_gate, b_h, b_c, b_b, b_gate, m_a, m_b = jnp.split(proj, SPLIT_IDX, axis=-1)
    u_a = a_val * jax.nn.sigmoid(a_glu)
    c_a, new_buf_a = causal_dwconv(buf_a, u_a, conv_a_w)
    c_a = layernorm(c_a + conv_a_b.astype(c_a.dtype), ln_a_g, ln_a_b)
    y_a = jax.nn.silu(c_a) * jax.nn.silu(a_gate)
    u_b = b_c * b_h
    c_b, new_buf_b = causal_dwconv(buf_b, u_b, conv_b_w)
    y_b = b_b * c_b * jax.nn.silu(b_gate)
    merged = (jax.nn.sigmoid(m_a) * jnp.einsum('btc,cd->btd', y_a, w_pa)
              + jax.nn.sigmoid(m_b) * jnp.einsum('btc,cd->btd', y_b, w_pb))
    h = h + jnp.einsum('btd,de->bte', merged, w_out)
    return h, new_buf_a, new_buf_b


def setup_inputs(seed: int = 0) -> dict:
    key = jax.random.key(seed)
    ks = jax.random.split(key, 18)
    f32 = jnp.float32
    x_prompt = jax.random.normal(ks[0], (BATCH, SEQ, D_MODEL), f32)
    x_sample = jax.random.normal(ks[1], (DEC_BATCH, DEC_SEQ, D_MODEL), f32)
    state_conv_a = 0.5 * jax.random.normal(ks[2], (DEPTH, DEC_BATCH, BUF_A, D_A), f32)
    state_conv_b = 0.5 * jax.random.normal(ks[3], (DEPTH, DEC_BATCH, BUF_B, D_B), f32)
    meta_tokens = jax.random.normal(ks[4], (N_META, D_MODEL), f32)
    norm_g = 1.0 + 0.02 * jax.random.normal(ks[5], (DEPTH, D_MODEL), f32)
    w_in = jax.random.normal(ks[6], (DEPTH, D_MODEL, D_IN), f32) * D_MODEL ** -0.5
    conv_a_w = jax.random.normal(ks[7], (DEPTH, CONV_A, D_A), f32) * CONV_A ** -0.5
    conv_a_b = 0.02 * jax.random.normal(ks[8], (DEPTH, D_A), f32)
    ln_a_g = 1.0 + 0.02 * jax.random.normal(ks[9], (DEPTH, D_A), f32)
    ln_a_b = 0.02 * jax.random.normal(ks[10], (DEPTH, D_A), f32)
    conv_b_w = jax.random.normal(ks[11], (DEPTH, CONV_B, D_B), f32) * CONV_B ** -0.5
    w_pa = jax.random.normal(ks[12], (DEPTH, D_A, D_MODEL), f32) * D_A ** -0.5
    w_pb = jax.random.normal(ks[13], (DEPTH, D_B, D_MODEL), f32) * D_B ** -0.5
    w_out = jax.random.normal(ks[14], (DEPTH, D_MODEL, D_MODEL), f32) * D_MODEL ** -0.5
    final_norm_g = 1.0 + 0.02 * jax.random.normal(ks[15], (D_MODEL,), f32)
    return {
        "x_prompt": x_prompt, "x_sample": x_sample,
        "state_conv_a": state_conv_a, "state_conv_b": state_conv_b,
        "meta_tokens": meta_tokens, "norm_g": norm_g, "w_in": w_in,
        "conv_a_w": conv_a_w, "conv_a_b": conv_a_b, "ln_a_g": ln_a_g, "ln_a_b": ln_a_b,
        "conv_b_w": conv_b_w, "w_pa": w_pa, "w_pb": w_pb, "w_out": w_out,
        "final_norm_g": final_norm_g,
    }


def reference(x_prompt, x_sample, state_conv_a, state_conv_b, meta_tokens, norm_g, w_in,
              conv_a_w, conv_a_b, ln_a_g, ln_a_b, conv_b_w, w_pa, w_pb, w_out, final_norm_g):
    n_p = x_prompt.shape[0]
    meta = jnp.broadcast_to(meta_tokens.astype(x_prompt.dtype)[None], (n_p, N_META, D_MODEL))
    h_p = jnp.concatenate([meta, x_prompt], axis=1)
    h_s = x_sample
    new_a_p, new_b_p, new_a_s, new_b_s = [], [], [], []
    for l in range(DEPTH):
        lw = (norm_g[l], w_in[l], conv_a_w[l], conv_a_b[l], ln_a_g[l], ln_a_b[l],
              conv_b_w[l], w_pa[l], w_pb[l], w_out[l])
        zero_a = jnp.zeros((n_p, BUF_A, D_A), h_p.dtype)
        zero_b = jnp.zeros((n_p, BUF_B, D_B), h_p.dtype)
        h_p, ba_p, bb_p = mixer_layer(h_p, zero_a, zero_b, *lw)
        h_s, ba_s, bb_s = mixer_layer(h_s, state_conv_a[l], state_conv_b[l], *lw)
        new_a_p.append(ba_p)
        new_b_p.append(bb_p)
        new_a_s.append(ba_s)
        new_b_s.append(bb_s)
    y_prompt = rmsnorm(h_p, final_norm_g)[:, N_META:, :]
    y_sample = rmsnorm(h_s, final_norm_g)
    new_conv_a_prompt = jnp.stack(new_a_p, axis=0)
    new_conv_b_prompt = jnp.stack(new_b_p, axis=0)
    new_conv_a_sample = jnp.stack(new_a_s, axis=0)
    new_conv_b_sample = jnp.stack(new_b_s, axis=0)
    return (y_prompt, y_sample, new_conv_a_prompt, new_conv_b_prompt, new_conv_a_sample, new_conv_b_sample)
```

```python
import functools

import jax
import jax.numpy as jnp
from jax import lax
from jax.experimental import pallas as pl
from jax.experimental.pallas import tpu as pltpu

N_META = 16
CONV_A = 31
CONV_B = 3
BUF_A = CONV_A - 1
BUF_B = CONV_B - 1
RMS_EPS = 1e-6
LN_EPS = 1e-5

LANES = 128
SUBLANES = 8
FRONT_PAD = 16
ROW_CHUNK = 208
CONV_TILE_ROWS = 104
HALO_A = 32
HALO_B = 8
SAMPLE_BLOCK = 16
VMEM_LIMIT = 56 * 1024 * 1024


def _sigmoid(x):
    return 0.5 * jnp.tanh(0.5 * x) + 0.5


def _silu(x):
    return x * _sigmoid(x)


def _rmsnorm(x, g):
    ms = jnp.mean(x * x, axis=-1, keepdims=True)
    return (x * lax.rsqrt(ms + RMS_EPS)) * g


def _layernorm(x, g, b):
    mu = jnp.mean(x, axis=-1, keepdims=True)
    xc = x - mu
    var = jnp.mean(xc * xc, axis=-1, keepdims=True)
    return (xc * lax.rsqrt(var + LN_EPS)) * g + b


def _proj(z_bf16, win_ref, group, d):
    return jnp.dot(z_bf16, win_ref[:, group * d:(group + 1) * d], preferred_element_type=jnp.float32)


def _conv_tile(src_ref, w_ref, first_row, rows, lane_block, ntaps):
    acc = None
    for k in range(ntaps):
        x = src_ref[lane_block, pl.ds(first_row + k, rows), :]
        term = x * w_ref[pl.ds(k, 1), pl.ds(lane_block * LANES, LANES)]
        acc = term if acc is None else acc + term
    return acc


def _store_slabs(dst_ref, first_row, x):
    for lb in range(x.shape[1] // LANES):
        dst_ref[lb, pl.ds(first_row, x.shape[0]), :] = x[:, lb * LANES:(lb + 1) * LANES]


def _prompt_layer_kernel(h_ref, ng_ref, win_ref, caw_ref, cab_ref, lng_ref, lnb_ref, cbw_ref,
                         wpa_ref, wpb_ref, wout_ref,
                         ho_ref, na_ref, nb_ref,
                         ua_scr, ub_scr, ca_scr, cb_scr):
    rc, d = h_ref.shape[1], h_ref.shape[2]
    i = pl.program_id(1)

    nlb = d // LANES

    @pl.when(i == 0)
    def _():
        ua_scr[:, 0:HALO_A, :] = jnp.zeros((nlb, HALO_A, LANES), jnp.float32)
        ub_scr[:, 0:HALO_B, :] = jnp.zeros((nlb, HALO_B, LANES), jnp.float32)

    h = h_ref[0]
    z = _rmsnorm(h, ng_ref[...]).astype(jnp.bfloat16)

    _store_slabs(ua_scr, HALO_A, _proj(z, win_ref, 0, d) * _sigmoid(_proj(z, win_ref, 1, d)))
    for r0 in range(0, rc, CONV_TILE_ROWS):
        for lb in range(nlb):
            ca_scr[r0:r0 + CONV_TILE_ROWS, lb * LANES:(lb + 1) * LANES] = _conv_tile(
                ua_scr, caw_ref, HALO_A - BUF_A + r0, CONV_TILE_ROWS, lb, CONV_A)
    c_a = _layernorm(ca_scr[...] + cab_ref[...], lng_ref[...], lnb_ref[...])
    y_a = (_silu(c_a) * _silu(_proj(z, win_ref, 2, d))).astype(jnp.bfloat16)

    _store_slabs(ub_scr, HALO_B, _proj(z, win_ref, 4, d) * _proj(z, win_ref, 3, d))
    for r0 in range(0, rc, CONV_TILE_ROWS):
        for lb in range(nlb):
            cb_scr[r0:r0 + CONV_TILE_ROWS, lb * LANES:(lb + 1) * LANES] = _conv_tile(
                ub_scr, cbw_ref, HALO_B - BUF_B + r0, CONV_TILE_ROWS, lb, CONV_B)
    y_b = (_proj(z, win_ref, 5, d) * cb_scr[...] * _silu(_proj(z, win_ref, 6, d))).astype(jnp.bfloat16)

    merged = (_sigmoid(_proj(z, win_ref, 7, d))
              * jnp.dot(y_a, wpa_ref[...], preferred_element_type=jnp.float32)
              + _sigmoid(_proj(z, win_ref, 8, d))
              * jnp.dot(y_b, wpb_ref[...], preferred_element_type=jnp.float32))
    ho_ref[0] = h + jnp.dot(merged.astype(jnp.bfloat16), wout_ref[...], preferred_element_type=jnp.float32)

    @pl.when(i == pl.num_programs(1) - 1)
    def _():
        for lb in range(nlb):
            na_ref[0, :, lb * LANES:(lb + 1) * LANES] = ua_scr[lb, HALO_A + rc - BUF_A:HALO_A + rc, :]
            nb_ref[0, :, lb * LANES:(lb + 1) * LANES] = ub_scr[lb, HALO_B + rc - BUF_B:HALO_B + rc, :]

    ua_scr[:, 0:HALO_A, :] = ua_scr[:, rc:rc + HALO_A, :]
    ub_scr[:, 0:HALO_B, :] = ub_scr[:, rc:rc + HALO_B, :]


def _resident(shape):
    return pl.BlockSpec(shape, lambda *_: (0,) * len(shape), pipeline_mode=pl.Buffered(1))


def _prompt_layer(h, ng, win, caw, cab, lng, lnb, cbw, wpa, wpb, wout):
    b, sp, d = h.shape
    rc = ROW_CHUNK
    assert sp % rc == 0 and rc % CONV_TILE_ROWS == 0 and CONV_TILE_ROWS % SUBLANES == 0
    assert rc >= HALO_A and d % LANES == 0
    row_block = pl.BlockSpec((1, rc, d), lambda bi, i: (bi, i, 0))
    return pl.pallas_call(
        _prompt_layer_kernel,
        grid=(b, sp // rc),
        in_specs=[row_block, _resident(ng.shape), _resident(win.shape), _resident(caw.shape),
                  _resident(cab.shape), _resident(lng.shape), _resident(lnb.shape), _resident(cbw.shape),
                  _resident(wpa.shape), _resident(wpb.shape), _resident(wout.shape)],
        out_specs=[row_block,
                   pl.BlockSpec((1, BUF_A, d), lambda bi, i: (bi, 0, 0)),
                   pl.BlockSpec((1, BUF_B, d), lambda bi, i: (bi, 0, 0))],
        out_shape=[jax.ShapeDtypeStruct((b, sp, d), jnp.float32),
                   jax.ShapeDtypeStruct((b, BUF_A, d), jnp.float32),
                   jax.ShapeDtypeStruct((b, BUF_B, d), jnp.float32)],
        scratch_shapes=[pltpu.VMEM((d // LANES, HALO_A + rc, LANES), jnp.float32),
                        pltpu.VMEM((d // LANES, HALO_B + rc, LANES), jnp.float32),
                        pltpu.VMEM((rc, d), jnp.float32),
                        pltpu.VMEM((rc, d), jnp.float32)],
        compiler_params=pltpu.CompilerParams(
            dimension_semantics=("arbitrary", "arbitrary"), vmem_limit_bytes=VMEM_LIMIT),
        name="prompt_layer",
    )(h, ng, win, caw, cab, lng, lnb, cbw, wpa, wpb, wout)


def _sample_layer_kernel(x_ref, sa_ref, sb_ref, ng_ref, win_ref, caw_ref, cab_ref, lng_ref, lnb_ref,
                         cbw_ref, wpa_ref, wpb_ref, wout_ref,
                         ho_ref, nsa_ref, nsb_ref,
                         ua_scr, ga_scr, ub_scr, gb_scr, sma_scr, smb_scr, ca_scr, cb_scr):
    n, d = x_ref.shape
    nblk = sa_ref.shape[0]
    j = pl.program_id(0)

    @pl.when(j == 0)
    def _():
        z = _rmsnorm(x_ref[...], ng_ref[...]).astype(jnp.bfloat16)
        ua_scr[...] = _proj(z, win_ref, 0, d) * _sigmoid(_proj(z, win_ref, 1, d))
        ga_scr[...] = _silu(_proj(z, win_ref, 2, d))
        ub_scr[...] = _proj(z, win_ref, 4, d) * _proj(z, win_ref, 3, d)
        gb_scr[...] = _proj(z, win_ref, 5, d) * _silu(_proj(z, win_ref, 6, d))
        sma_scr[...] = _sigmoid(_proj(z, win_ref, 7, d))
        smb_scr[...] = _sigmoid(_proj(z, win_ref, 8, d))

    for t in range(nblk):
        row = pl.ds(j * nblk + t, 1)
        ua = ua_scr[row, :]
        hist_a = sa_ref[t]
        ca_scr[row, :] = (jnp.sum(hist_a * caw_ref[0:BUF_A, :], axis=0, keepdims=True)
                          + ua * caw_ref[BUF_A:CONV_A, :])
        nsa_ref[t, 0:BUF_A - 1, :] = sa_ref[t, 1:BUF_A, :]
        nsa_ref[t, BUF_A - 1:BUF_A, :] = ua
        ub = ub_scr[row, :]
        hist_b = sb_ref[t]
        cb_scr[row, :] = (jnp.sum(hist_b * cbw_ref[0:BUF_B, :], axis=0, keepdims=True)
                          + ub * cbw_ref[BUF_B:CONV_B, :])
        nsb_ref[t, 0:BUF_B - 1, :] = sb_ref[t, 1:BUF_B, :]
        nsb_ref[t, BUF_B - 1:BUF_B, :] = ub

    @pl.when(j == pl.num_programs(0) - 1)
    def _():
        c_a = _layernorm(ca_scr[...] + cab_ref[...], lng_ref[...], lnb_ref[...])
        y_a = (_silu(c_a) * ga_scr[...]).astype(jnp.bfloat16)
        y_b = (gb_scr[...] * cb_scr[...]).astype(jnp.bfloat16)
        merged = (sma_scr[...] * jnp.dot(y_a, wpa_ref[...], preferred_element_type=jnp.float32)
                  + smb_scr[...] * jnp.dot(y_b, wpb_ref[...], preferred_element_type=jnp.float32))
        ho_ref[...] = x_ref[...] + jnp.dot(merged.astype(jnp.bfloat16), wout_ref[...],
                                           preferred_element_type=jnp.float32)


def _sample_layer(x, sa, sb, ng, win, caw, cab, lng, lnb, cbw, wpa, wpb, wout):
    n, d = x.shape
    nblk = SAMPLE_BLOCK
    assert n % nblk == 0
    tok_vec = pltpu.VMEM((n, d), jnp.float32)
    return pl.pallas_call(
        _sample_layer_kernel,
        grid=(n // nblk,),
        in_specs=[_resident(x.shape),
                  pl.BlockSpec((nblk, BUF_A, d), lambda j: (j, 0, 0)),
                  pl.BlockSpec((nblk, BUF_B, d), lambda j: (j, 0, 0)),
                  _resident(ng.shape), _resident(win.shape), _resident(caw.shape), _resident(cab.shape),
                  _resident(lng.shape), _resident(lnb.shape), _resident(cbw.shape),
                  _resident(wpa.shape), _resident(wpb.shape), _resident(wout.shape)],
        out_specs=[pl.BlockSpec((n, d), lambda j: (0, 0)),
                   pl.BlockSpec((nblk, BUF_A, d), lambda j: (j, 0, 0)),
                   pl.BlockSpec((nblk, BUF_B, d), lambda j: (j, 0, 0))],
        out_shape=[jax.ShapeDtypeStruct((n, d), jnp.float32),
                   jax.ShapeDtypeStruct(sa.shape, jnp.float32),
                   jax.ShapeDtypeStruct(sb.shape, jnp.float32)],
        scratch_shapes=[tok_vec] * 8,
        compiler_params=pltpu.CompilerParams(
            dimension_semantics=("arbitrary",), vmem_limit_bytes=VMEM_LIMIT),
        name="sample_layer",
    )(x, sa, sb, ng, win, caw, cab, lng, lnb, cbw, wpa, wpb, wout)


def _final_norm_kernel(h_ref, g_ref, o_ref):
    o_ref[...] = _rmsnorm(h_ref[...], g_ref[...])


def _final_norm(h2d, g, rows):
    n, d = h2d.shape
    assert n % rows == 0
    return pl.pallas_call(
        _final_norm_kernel,
        grid=(n // rows,),
        in_specs=[pl.BlockSpec((rows, d), lambda i: (i, 0)), pl.BlockSpec((1, d), lambda i: (0, 0))],
        out_specs=pl.BlockSpec((rows, d), lambda i: (i, 0)),
        out_shape=jax.ShapeDtypeStruct((n, d), jnp.float32),
        name="final_norm",
    )(h2d, g)


def kernel(x_prompt, x_sample, state_conv_a, state_conv_b, meta_tokens, norm_g, w_in, conv_a_w, conv_a_b,
           ln_a_g, ln_a_b, conv_b_w, w_pa, w_pb, w_out, final_norm_g):
    b, s, d = x_prompt.shape
    depth = w_in.shape[0]
    n_s = x_sample.shape[0]
    bf16 = jnp.bfloat16

    front = jnp.zeros((b, FRONT_PAD, d), x_prompt.dtype)
    meta = jnp.broadcast_to(meta_tokens.astype(x_prompt.dtype)[None], (b, N_META, d))
    h_p = jnp.concatenate([front, meta, x_prompt], axis=1)
    h_s = x_sample.reshape(n_s, d)

    new_a_p, new_b_p, new_a_s, new_b_s = [], [], [], []
    for l in range(depth):
        lw = (norm_g[l][None], w_in[l].astype(bf16), conv_a_w[l], conv_a_b[l][None], ln_a_g[l][None],
              ln_a_b[l][None], conv_b_w[l], w_pa[l].astype(bf16), w_pb[l].astype(bf16), w_out[l].astype(bf16))
        h_p, na, nb = _prompt_layer(h_p, *lw)
        h_s, nsa, nsb = _sample_layer(h_s, state_conv_a[l], state_conv_b[l], *lw)
        new_a_p.append(na)
        new_b_p.append(nb)
        new_a_s.append(nsa)
        new_b_s.append(nsb)

    sp = h_p.shape[1]
    y_p = _final_norm(h_p.reshape(b * sp, d), final_norm_g[None], ROW_CHUNK).reshape(b, sp, d)
    y_prompt = y_p[:, FRONT_PAD + N_META:, :]
    y_sample = _final_norm(h_s, final_norm_g[None], n_s).reshape(n_s, 1, d)
    return (y_prompt, y_sample, jnp.stack(new_a_p, axis=0), jnp.stack(new_b_p, axis=0),
            jnp.stack(new_a_s, axis=0), jnp.stack(new_b_s, axis=0))
```
